```python
import math
import jax
import jax.numpy as jnp
from jax import lax
import numpy as np


D_MODEL = 1024
BATCH = 2
SEQ = 8192
DEPTH = 4
DEC_BATCH = 16
DEC_SEQ = 4096
PAST_LEN = 128

BRANCH_WIDTH = 512
N_BRANCH = 3
HY_WIDTH = BRANCH_WIDTH
HY_ORDER = 2
HY_COLS = (HY_ORDER + 1) * HY_WIDTH
SHORT_CONV = 3
FILT_EMB = 33
FILT_HIDDEN = 64
FILT_DECAY_TARGET = 1e-2
FAST_DECAY_PCT = 0.3
SLOW_DECAY_PCT = 1.5
MIN_DECAY = math.log(FILT_DECAY_TARGET) / SLOW_DECAY_PCT
MAX_DECAY = math.log(FILT_DECAY_TARGET) / FAST_DECAY_PCT
DIL_CONFIGS = ((128, 1), (512, 4), (2048, 16))
DIL_GROUPS = 3
DIL_HEADS = 4
DIL_HEAD_DIM = 128
DIL_BLOCK = 64
DIL_COLS = DIL_GROUPS * DIL_HEADS * DIL_HEAD_DIM
WIN_HALF = 128
WIN_Q_HEADS = 8
WIN_KV_HEADS = 2
WIN_GROUP = WIN_Q_HEADS // WIN_KV_HEADS
WIN_HEAD_DIM = 64
WIN_BLOCK = 128
WIN_Q_COLS = WIN_Q_HEADS * WIN_HEAD_DIM
WIN_KV_COLS = WIN_KV_HEADS * WIN_HEAD_DIM
GATE_COLS = N_BRANCH * D_MODEL
O_DQ = HY_COLS
O_DK = O_DQ + DIL_COLS
O_DV = O_DK + DIL_COLS
O_WQ = O_DV + DIL_COLS
O_WK = O_WQ + WIN_Q_COLS
O_WV = O_WK + WIN_KV_COLS
O_GATE = O_WV + WIN_KV_COLS
IN_COLS = O_GATE + GATE_COLS
IN_SPLITS = (O_DQ, O_DK, O_DV, O_WQ, O_WK, O_WV, O_GATE)
ROPE_THETA = 10000.0
RMS_EPS = 1e-6
NEG_INF = -1e30
D_FF_DENSE = 2816
N_EXPERTS = 8
TOP_K = 2
D_FF_EXPERT = 3584
N_DENSE = (DEPTH + 1) // 2
N_MOE = DEPTH // 2

kernel_name = 'hybrid_hyena_dilated_window_encoder'


def rmsnorm(x, g):
    xf = x.astype(jnp.float32)
    y = xf * lax.rsqrt(jnp.mean(xf * xf, axis=-1, keepdims=True) + RMS_EPS)
    return (y * g.astype(jnp.float32)).astype(x.dtype)


def rope(x):
    L = x.shape[1]
    half = x.shape[-1] // 2
    inv = ROPE_THETA ** (-jnp.arange(half, dtype=jnp.float32) / half)
    ang = jnp.arange(L, dtype=jnp.float32)[:, None] * inv[None, :]
    bshape = (L,) + (1,) * (x.ndim - 3) + (half,)
    cos = jnp.cos(ang).reshape(bshape)
    sin = jnp.sin(ang).reshape(bshape)
    xf = x.astype(jnp.float32)
    x1, x2 = xf[..., :half], xf[..., half:]
    return jnp.concatenate([x1 * cos - x2 * sin, x2 * cos + x1 * sin], axis=-1).astype(x.dtype)


def banded_attention(q, k, v, half_window, block, sink=None):
    n, L, hk, g, dh = q.shape
    nb = -(-L // block)
    lp = nb * block
    ns = -(-half_window // block)
    qp = jnp.pad(q, ((0, 0), (0, lp - L), (0, 0), (0, 0), (0, 0))).reshape(n, nb, block, hk, g, dh)
    kpad = ((0, 0), (ns * block, lp - L + ns * block), (0, 0), (0, 0))
    kb = jnp.pad(k, kpad).reshape(n, nb + 2 * ns, block, hk, dh)
    vb = jnp.pad(v, kpad).reshape(n, nb + 2 * ns, block, hk, dh)
    kw = jnp.concatenate([kb[:, j:j + nb] for j in range(2 * ns + 1)], axis=2)
    vw = jnp.concatenate([vb[:, j:j + nb] for j in range(2 * ns + 1)], axis=2)
    width = (2 * ns + 1) * block
    qpos = jnp.arange(nb)[:, None] * block + jnp.arange(block)[None, :]
    kpos = jnp.arange(nb)[:, None] * block - ns * block + jnp.arange(width)[None, :]
    rel = kpos[:, None, :] - qpos[:, :, None]
    valid = (jnp.abs(rel) <= half_window) & (kpos[:, None, :] >= 0) & (kpos[:, None, :] < L)
    s = jnp.einsum('nbqhgd,nbkhd->nbhgqk', qp, kw, preferred_element_type=jnp.float32) * (dh ** -0.5)
    s = jnp.where(valid[None, :, None, None], s, NEG_INF)
    m = jnp.max(s, axis=-1)
    if sink is not None:
        sk = sink.astype(jnp.float32).reshape(hk, g)[None, None, :, :, None]
        m = jnp.maximum(m, sk)
    p = jnp.exp(s - m[..., None])
    l = jnp.sum(p, axis=-1)
    if sink is not None:
        l = l + jnp.exp(sk - m)
    o = jnp.einsum('nbhgqk,nbkhd->nbhgqd', p, vw.astype(jnp.float32)) / l[..., None]
    o = o.transpose(0, 1, 4, 2, 3, 5).reshape(n, lp, hk, g, dh)[:, :L]
    logden = (m + jnp.log(l)).transpose(0, 1, 4, 2, 3).reshape(n, lp, hk, g)[:, :L]
    return o, logden


def stride_gather(t, dil):
    b, L, h, dh = t.shape
    return t.reshape(b, L // dil, dil, h, dh).transpose(0, 2, 1, 3, 4).reshape(b * dil, L // dil, h, dh)


def stride_scatter(t, b, dil):
    ls = t.shape[1]
    rest = t.shape[2:]
    t = t.reshape((b, dil, ls) + rest)
    perm = (0, 2, 1) + tuple(range(3, t.ndim))
    return t.transpose(perm).reshape((b, ls * dil) + rest)


def dilated_mixture(q, k, v):
    b, L = q.shape[:2]
    outs, logdens = [], []
    for gi, (window, dil) in enumerate(DIL_CONFIGS):
        qs = stride_gather(q[:, :, gi], dil)[:, :, :, None]
        ks = stride_gather(k[:, :, gi], dil)
        vs = stride_gather(v[:, :, gi], dil)
        o, ld = banded_attention(qs, ks, vs, window // (2 * dil), DIL_BLOCK)
        outs.append(stride_scatter(o[:, :, :, 0], b, dil))
        logdens.append(stride_scatter(ld[..., 0], b, dil))
    wts = jax.nn.softmax(jnp.stack(logdens, axis=0), axis=0)
    out = jnp.sum(wts[..., None] * jnp.stack(outs, axis=0), axis=0)
    return out.reshape(b, L, DIL_HEADS * DIL_HEAD_DIM).astype(q.dtype)


def hyena_spectra(L, w1, b1, w2, b2, w3, b3, freq, w_out):
    f32 = jnp.float32
    t = jnp.linspace(0.0, 1.0, L, dtype=f32)
    bands = (FILT_EMB - 1) // 2
    fb = jnp.linspace(1e-4, bands - 1, bands, dtype=f32)
    ang = (2.0 * math.pi / L) * jnp.arange(L, dtype=f32)[:, None] * fb[None, :]
    z = jnp.concatenate([t[:, None], jnp.cos(ang), -jnp.sin(ang)], axis=-1)
    fr = freq.astype(f32)
    hdn = jnp.sin(fr * (z @ w1.astype(f32) + b1.astype(f32)))
    hdn = jnp.sin(fr * (hdn @ w2.astype(f32) + b2.astype(f32)))
    hdn = jnp.sin(fr * (hdn @ w3.astype(f32) + b3.astype(f32)))
    kern = (hdn @ w_out.astype(f32)).reshape(L, 2, HY_ORDER, HY_WIDTH)
    deltas = jnp.abs(jnp.linspace(MIN_DECAY, MAX_DECAY, HY_WIDTH, dtype=f32))
    kern = kern * jnp.exp(-t[:, None] * deltas[None, :])[:, None, None, :]
    kf, kb = kern[:, 0], kern[:, 1]
    kfull = jnp.concatenate([kf, jnp.zeros((1, HY_ORDER, HY_WIDTH), f32), kb[:0:-1]], axis=0)
    kfull = kfull / jnp.sum(jnp.abs(kfull), axis=0, keepdims=True)
    return jnp.fft.rfft(kfull, axis=0)


def short_conv(u, w, b):
    L = u.shape[1]
    r = SHORT_CONV // 2
    up = jnp.pad(u, ((0, 0), (r, r), (0, 0)))
    y = b
    for j in range(SHORT_CONV):
        y = y + up[:, j:j + L] * w[j]
    return y


def long_conv(z, spec):
    L = z.shape[1]
    zf = jnp.fft.rfft(z.astype(jnp.float32), n=2 * L, axis=1)
    return jnp.fft.irfft(zf * spec[None], n=2 * L, axis=1)[:, :L].astype(z.dtype)


def hyena_mixer(u_proj, conv_w, conv_b, spec, bias):
    u = short_conv(u_proj, conv_w, conv_b)
    parts = jnp.split(u, HY_ORDER + 1, axis=-1)
    z = parts[0]
    for o in range(HY_ORDER):
        z = parts[o + 1] * (long_conv(z, spec[:, o]) + bias[o] * z)
    return z


def swiglu(h, w1, w3, w2):
    return (jax.nn.silu(h @ w1) * (h @ w3)) @ w2


def moe_ffn(h, router_w, w1, w3, w2):
    logits = (h @ router_w).astype(jnp.float32)
    top_v, top_i = lax.top_k(logits, TOP_K)
    gate = jax.nn.softmax(top_v, axis=-1)
    combine = jnp.sum(jax.nn.one_hot(top_i, N_EXPERTS, dtype=jnp.float32) * gate[..., None], axis=-2)
    out = jnp.zeros(h.shape, jnp.float32)
    for e in range(N_EXPERTS):
        out = out + combine[..., e:e + 1] * swiglu(h, w1[e], w3[e], w2[e]).astype(jnp.float32)
    return out.astype(h.dtype)


def mixer_block(h, p, l, spec):
    b, L, _ = h.shape
    proj = h @ p['w_in'][l]
    hy_in, dq, dk, dv, wq, wk, wv, gates = jnp.split(proj, IN_SPLITS, axis=-1)
    y_hy = hyena_mixer(hy_in, p['hy_conv_w'][l], p['hy_conv_b'][l], spec, p['hy_bias'][l])
    dshape = (b, L, DIL_GROUPS, DIL_HEADS, DIL_HEAD_DIM)
    dqn = rope(rmsnorm(dq.reshape(dshape), p['dil_q_norm'][l][:, None, :]))
    dkn = rope(rmsnorm(dk.reshape(dshape), p['dil_k_norm'][l][:, None, :]))
    y_dil = dilated_mixture(dqn, dkn, dv.reshape(dshape))
    wqn = rope(rmsnorm(wq.reshape(b, L, WIN_Q_HEADS, WIN_HEAD_DIM), p['win_q_norm'][l]))
    wkn = rope(rmsnorm(wk.reshape(b, L, WIN_KV_HEADS, WIN_HEAD_DIM), p['win_k_norm'][l]))
    o_win, _ = banded_attention(wqn.reshape(b, L, WIN_KV_HEADS, WIN_GROUP, WIN_HEAD_DIM), wkn,
                                wv.reshape(b, L, WIN_KV_HEADS, WIN_HEAD_DIM), WIN_HALF, WIN_BLOCK,
                                sink=p['win_sink'][l])
    y_win = o_win.reshape(b, L, WIN_Q_COLS).astype(h.dtype)
    branches = (y_hy, y_dil, y_win)
    merged = None
    for i in range(N_BRANCH):
        g = jax.nn.sigmoid(gates[..., i * D_MODEL:(i + 1) * D_MODEL])
        term = g * (branches[i] @ p['w_branch'][l, i])
        merged = term if merged is None else merged + term
    return merged @ p['w_out'][l]


def run_trunk(x, p):
    L = x.shape[1]
    for l in range(DEPTH):
        spec = hyena_spectra(L, p['filt_w1'][l], p['filt_b1'][l], p['filt_w2'][l], p['filt_b2'][l],
                             p['filt_w3'][l], p['filt_b3'][l], p['filt_freq'][l], p['filt_out'][l])
        x = x + mixer_block(rmsnorm(x, p['mix_norm'][l]), p, l, spec)
        h = rmsnorm(x, p['ffn_norm'][l])
        if l % 2 == 0:
            j = l // 2
            f = swiglu(h, p['dense_w1'][j], p['dense_w3'][j], p['dense_w2'][j])
        else:
            j = l // 2
            f = moe_ffn(h, p['router_w'][j], p['moe_w1'][j], p['moe_w3'][j], p['moe_w2'][j])
        x = x + f
    return x


def setup_inputs(seed: int = 0) -> dict:
    key = jax.random.key(seed)
    ks = iter(jax.random.split(key, 32))
    f32 = jnp.float32

    def nrm(shape, scale):
        return jax.random.normal(next(ks), shape, f32) * scale

    def gain(shape):
        return 1.0 + 0.05 * jax.random.normal(next(ks), shape, f32)

    return {
        'x_prompt': nrm((BATCH, SEQ, D_MODEL), 1.0),
        'x_sample': nrm((DEC_BATCH, DEC_SEQ, D_MODEL), 1.0),
        'mix_norm': gain((DEPTH, D_MODEL)),
        'w_in': nrm((DEPTH, D_MODEL, IN_COLS), D_MODEL ** -0.5),
        'hy_conv_w': nrm((DEPTH, SHORT_CONV, HY_COLS), SHORT_CONV ** -0.5),
        'hy_conv_b': nrm((DEPTH, HY_COLS), 0.02),
        'filt_w1': nrm((DEPTH, FILT_EMB, FILT_HIDDEN), FILT_EMB ** -0.5),
        'filt_b1': nrm((DEPTH, FILT_HIDDEN), 0.1),
        'filt_w2': nrm((DEPTH, FILT_HIDDEN, FILT_HIDDEN), FILT_HIDDEN ** -0.5),
        'filt_b2': nrm((DEPTH, FILT_HIDDEN), 0.1),
        'filt_w3': nrm((DEPTH, FILT_HIDDEN, FILT_HIDDEN), FILT_HIDDEN ** -0.5),
        'filt_b3': nrm((DEPTH, FILT_HIDDEN), 0.1),
        'filt_freq': gain((DEPTH, FILT_HIDDEN)),
        'filt_out': nrm((DEPTH, FILT_HIDDEN, 2 * HY_ORDER * HY_WIDTH), FILT_HIDDEN ** -0.5),
        'hy_bias': nrm((DEPTH, HY_ORDER, HY_WIDTH), 0.5),
        'dil_q_norm': gain((DEPTH, DIL_GROUPS, DIL_HEAD_DIM)),
        'dil_k_norm': gain((DEPTH, DIL_GROUPS, DIL_HEAD_DIM)),
        'win_q_norm': gain((DEPTH, WIN_HEAD_DIM)),
        'win_k_norm': gain((DEPTH, WIN_HEAD_DIM)),
        'win_sink': nrm((DEPTH, WIN_Q_HEADS), 0.5),
        'w_branch': nrm((DEPTH, N_BRANCH, BRANCH_WIDTH, D_MODEL), BRANCH_WIDTH ** -0.5),
        'w_out': nrm((DEPTH, D_MODEL, D_MODEL), D_MODEL ** -0.5),
        'ffn_norm': gain((DEPTH, D_MODEL)),
        'dense_w1': nrm((N_DENSE, D_MODEL, D_FF_DENSE), D_MODEL ** -0.5),
        'dense_w3': nrm((N_DENSE, D_MODEL, D_FF_DENSE), D_MODEL ** -0.5),
        'dense_w2': nrm((N_DENSE, D_FF_DENSE, D_MODEL), D_FF_DENSE ** -0.5),
        'router_w': nrm((N_MOE, D_MODEL, N_EXPERTS), D_MODEL ** -0.5),
        'moe_w1': nrm((N_MOE, N_EXPERTS, D_MODEL, D_FF_EXPERT), D_MODEL ** -0.5),
        'moe_w3': nrm((N_MOE, N_EXPERTS, D_MODEL, D_FF_EXPERT), D_MODEL ** -0.5),
        'moe_w2': nrm((N_MOE, N_EXPERTS, D_FF_EXPERT, D_MODEL), D_FF_EXPERT ** -0.5),
    }


def reference(x_prompt, x_sample, mix_norm, w_in, hy_conv_w, hy_conv_b, filt_w1, filt_b1, filt_w2,
              filt_b2, filt_w3, filt_b3, filt_freq, filt_out, hy_bias, dil_q_norm, dil_k_norm,
              win_q_norm, win_k_norm, win_sink, w_branch, w_out, ffn_norm, dense_w1, dense_w3,
              dense_w2, router_w, moe_w1, moe_w3, moe_w2):
    p = dict(mix_norm=mix_norm, w_in=w_in, hy_conv_w=hy_conv_w, hy_conv_b=hy_conv_b,
             filt_w1=filt_w1, filt_b1=filt_b1, filt_w2=filt_w2, filt_b2=filt_b2, filt_w3=filt_w3,
             filt_b3=filt_b3, filt_freq=filt_freq, filt_out=filt_out, hy_bias=hy_bias,
             dil_q_norm=dil_q_norm, dil_k_norm=dil_k_norm, win_q_norm=win_q_norm,
             win_k_norm=win_k_norm, win_sink=win_sink, w_branch=w_branch, w_out=w_out,
             ffn_norm=ffn_norm, dense_w1=dense_w1, dense_w3=dense_w3, dense_w2=dense_w2,
             router_w=router_w, moe_w1=moe_w1, moe_w3=moe_w3, moe_w2=moe_w2)
    y_prompt = run_trunk(x_prompt, p)
    y_sample = run_trunk(x_sample, p)
    return (y_prompt, y_sample)
```

```python
import functools
import math

import jax
import jax.numpy as jnp
from jax import lax
from jax.experimental import pallas as pl
from jax.experimental.pallas import tpu as pltpu

D_MODEL = 1024
DEPTH = 4
BRANCH_WIDTH = 512
N_BRANCH = 3
HY_WIDTH = 512
HY_ORDER = 2
HY_COLS = (HY_ORDER + 1) * HY_WIDTH
SHORT_CONV = 3
FILT_EMB = 33
FILT_DECAY_TARGET = 1e-2
MIN_DECAY = math.log(FILT_DECAY_TARGET) / 1.5
MAX_DECAY = math.log(FILT_DECAY_TARGET) / 0.3
DIL_CONFIGS = ((128, 1), (512, 4), (2048, 16))
DIL_GROUPS = 3
DIL_HEADS = 4
DIL_HEAD_DIM = 128
DIL_BLOCK = 64
DIL_COLS = DIL_GROUPS * DIL_HEADS * DIL_HEAD_DIM
WIN_HALF = 128
WIN_Q_HEADS = 8
WIN_KV_HEADS = 2
WIN_GROUP = WIN_Q_HEADS // WIN_KV_HEADS
WIN_HEAD_DIM = 64
WIN_BLOCK = 128
WIN_Q_COLS = WIN_Q_HEADS * WIN_HEAD_DIM
WIN_KV_COLS = WIN_KV_HEADS * WIN_HEAD_DIM
GATE_COLS = N_BRANCH * D_MODEL
ROPE_THETA = 10000.0
RMS_EPS = 1e-6
NEG_INF = -1e30
N_EXPERTS = 8
TOP_K = 2

C_GATE = 0
C_HY = C_GATE + GATE_COLS
C_DQ = C_HY + HY_COLS
C_DK = C_DQ + DIL_COLS
C_DV = C_DK + DIL_COLS
C_WQ = C_DV + DIL_COLS
C_WK = C_WQ + WIN_Q_COLS
C_WV = C_WK + WIN_KV_COLS
IN_COLS = C_WV + WIN_KV_COLS

VMEM_LIMIT = 48 * 1024 * 1024
BF16 = jnp.bfloat16
F32 = jnp.float32


def _rms_rows(x, g):
    ms = jnp.mean(x * x, axis=-1, keepdims=True)
    return x * lax.rsqrt(ms + RMS_EPS) * g


def _norm_proj_body(x_ref, g_ref, w_ref, o_ref, h_ref):
    @pl.when(pl.program_id(1) == 0)
    def _():
        h_ref[...] = _rms_rows(x_ref[...], g_ref[...]).astype(BF16)

    o_ref[...] = jnp.dot(h_ref[...], w_ref[...], preferred_element_type=F32).astype(o_ref.dtype)


def norm_proj(x, g, w, *, tm=1024, tn=768, interpret=False):
    t, d = x.shape
    n = w.shape[1]
    tm = min(tm, t)
    return pl.pallas_call(
        _norm_proj_body,
        grid=(t // tm, n // tn),
        in_specs=[
            pl.BlockSpec((tm, d), lambda i, j: (i, 0)),
            pl.BlockSpec((1, d), lambda i, j: (0, 0)),
            pl.BlockSpec((d, tn), lambda i, j: (0, j)),
        ],
        out_specs=pl.BlockSpec((tm, tn), lambda i, j: (i, j)),
        out_shape=jax.ShapeDtypeStruct((t, n), BF16),
        scratch_shapes=[pltpu.VMEM((tm, d), BF16)],
        compiler_params=pltpu.CompilerParams(
            dimension_semantics=("parallel", "arbitrary"), vmem_limit_bytes=VMEM_LIMIT),
        name="norm_proj",
        interpret=interpret,
    )(x, g.reshape(1, d), w)


def _merge_body(x_ref, g0_ref, g1_ref, g2_ref, y0_ref, y1_ref, y2_ref, wb_ref, wo_ref, o_ref):
    merged = None
    for i, (g_ref, y_ref) in enumerate(((g0_ref, y0_ref), (g1_ref, y1_ref), (g2_ref, y2_ref))):
        br = jnp.dot(y_ref[...], wb_ref[i], preferred_element_type=F32)
        term = jax.nn.sigmoid(g_ref[...].astype(F32)) * br
        merged = term if merged is None else merged + term
    o_ref[...] = x_ref[...] + jnp.dot(merged.astype(BF16), wo_ref[...], preferred_element_type=F32)


def merge_out(x, proj, ys, wb, wo, *, tm=512, interpret=False):
    t, d = x.shape
    tm = min(tm, t)
    gate_specs = [
        pl.BlockSpec((tm, d), functools.partial(lambda i, c: (i, c), c=C_GATE // d + k)) for k in range(N_BRANCH)
    ]
    y_specs = [pl.BlockSpec((tm, BRANCH_WIDTH), lambda i: (i, 0)) for _ in range(N_BRANCH)]
    return pl.pallas_call(
        _merge_body,
        grid=(t // tm,),
        in_specs=[pl.BlockSpec((tm, d), lambda i: (i, 0))] + gate_specs + y_specs + [
            pl.BlockSpec((N_BRANCH, BRANCH_WIDTH, d), lambda i: (0, 0, 0)),
            pl.BlockSpec((d, d), lambda i: (0, 0)),
        ],
        out_specs=pl.BlockSpec((tm, d), lambda i: (i, 0)),
        out_shape=jax.ShapeDtypeStruct((t, d), F32),
        compiler_params=pltpu.CompilerParams(
            dimension_semantics=("parallel",), vmem_limit_bytes=VMEM_LIMIT),
        name="merge_out",
        interpret=interpret,
    )(x, proj, proj, proj, *ys, wb, wo)


def _ffn_body(x_ref, g_ref, w1_ref, w3_ref, w2_ref, o_ref, h_ref, acc_ref):
    j = pl.program_id(1)

    @pl.when(j == 0)
    def _():
        h_ref[...] = _rms_rows(x_ref[...], g_ref[...]).astype(BF16)
        acc_ref[...] = x_ref[...]

    h = h_ref[...]
    a = jnp.dot(h, w1_ref[...], preferred_element_type=F32)
    b = jnp.dot(h, w3_ref[...], preferred_element_type=F32)
    act = (a * jax.nn.sigmoid(a) * b).astype(BF16)
    acc_ref[...] += jnp.dot(act, w2_ref[...], preferred_element_type=F32)

    @pl.when(j == pl.num_programs(1) - 1)
    def _():
        o_ref[...] = acc_ref[...]


def ffn_dense(x, g, w1, w3, w2, *, tm=1024, tf=256, interpret=False):
    t, d = x.shape
    f = w1.shape[1]
    tm = min(tm, t)
    return pl.pallas_call(
        _ffn_body,
        grid=(t // tm, f // tf),
        in_specs=[
            pl.BlockSpec((tm, d), lambda i, j: (i, 0)),
            pl.BlockSpec((1, d), lambda i, j: (0, 0)),
            pl.BlockSpec((d, tf), lambda i, j: (0, j)),
            pl.BlockSpec((d, tf), lambda i, j: (0, j)),
            pl.BlockSpec((tf, d), lambda i, j: (j, 0)),
        ],
        out_specs=pl.BlockSpec((tm, d), lambda i, j: (i, 0)),
        out_shape=jax.ShapeDtypeStruct((t, d), F32),
        scratch_shapes=[pltpu.VMEM((tm, d), BF16), pltpu.VMEM((tm, d), F32)],
        compiler_params=pltpu.CompilerParams(
            dimension_semantics=("parallel", "arbitrary"), vmem_limit_bytes=VMEM_LIMIT),
        name="ffn_dense",
        interpret=interpret,
    )(x, g.reshape(1, d), w1, w3, w2)


E_PAD = 128


def _top2_combine(logits):
    lane = lax.broadcasted_iota(jnp.int32, logits.shape, 1)
    m1 = jnp.max(logits, axis=-1, keepdims=True)
    i1 = jnp.min(jnp.where(logits == m1, lane, E_PAD), axis=-1, keepdims=True)
    rest = jnp.where(lane == i1, NEG_INF, logits)
    m2 = jnp.max(rest, axis=-1, keepdims=True)
    i2 = jnp.min(jnp.where(rest == m2, lane, E_PAD), axis=-1, keepdims=True)
    e2 = jnp.exp(m2 - m1)
    den = 1.0 + e2
    return jnp.where(lane == i1, 1.0 / den, 0.0) + jnp.where(lane == i2, e2 / den, 0.0)


def _moe_body(x_ref, g_ref, rw_ref, w1_ref, w3_ref, w2_ref, o_ref, h_ref, acc_ref, c_ref):
    e = pl.program_id(1)
    j = pl.program_id(2)

    @pl.when((e == 0) & (j == 0))
    def _():
        hn = _rms_rows(x_ref[...], g_ref[...])
        h_ref[...] = hn.astype(BF16)
        acc_ref[...] = jnp.zeros_like(acc_ref)
        logits = jnp.dot(hn, rw_ref[...], preferred_element_type=F32, precision=lax.Precision.HIGHEST)
        lane = lax.broadcasted_iota(jnp.int32, logits.shape, 1)
        c_ref[...] = _top2_combine(jnp.where(lane < N_EXPERTS, logits, NEG_INF))

    lane = lax.broadcasted_iota(jnp.int32, c_ref.shape, 1)
    ce = jnp.sum(jnp.where(lane == e, c_ref[...], 0.0), axis=-1, keepdims=True)
    h = h_ref[...]
    a = jnp.dot(h, w1_ref[0], preferred_element_type=F32)
    b = jnp.dot(h, w3_ref[0], preferred_element_type=F32)
    act = (a * jax.nn.sigmoid(a) * b).astype(BF16)
    acc_ref[...] += ce * jnp.dot(act, w2_ref[0], preferred_element_type=F32)

    @pl.when((e == pl.num_programs(1) - 1) & (j == pl.num_programs(2) - 1))
    def _():
        o_ref[...] = x_ref[...] + acc_ref[...]


def ffn_moe(x, g, rw, w1, w3, w2, *, tm=1024, tf=512, interpret=False):
    t, d = x.shape
    ne, _, f = w1.shape
    tm = min(tm, t)
    rw_pad = jnp.zeros((d, E_PAD), F32).at[:, :ne].set(rw)
    return pl.pallas_call(
        _moe_body,
        grid=(t // tm, ne, f // tf),
        in_specs=[
            pl.BlockSpec((tm, d), lambda i, e, j: (i, 0)),
            pl.BlockSpec((1, d), lambda i, e, j: (0, 0)),
            pl.BlockSpec((d, E_PAD), lambda i, e, j: (0, 0)),
            pl.BlockSpec((1, d, tf), lambda i, e, j: (e, 0, j)),
            pl.BlockSpec((1, d, tf), lambda i, e, j: (e, 0, j)),
            pl.BlockSpec((1, tf, d), lambda i, e, j: (e, j, 0)),
        ],
        out_specs=pl.BlockSpec((tm, d), lambda i, e, j: (i, 0)),
        out_shape=jax.ShapeDtypeStruct((t, d), F32),
        scratch_shapes=[pltpu.VMEM((tm, d), BF16), pltpu.VMEM((tm, d), F32), pltpu.VMEM((tm, E_PAD), F32)],
        compiler_params=pltpu.CompilerParams(
            dimension_semantics=("parallel", "arbitrary", "arbitrary"), vmem_limit_bytes=VMEM_LIMIT),
        name="ffn_moe",
        interpret=interpret,
    )(x, g.reshape(1, d), rw_pad, w1, w3, w2)


def _rmsnorm(x, g):
    xf = x.astype(F32)
    return xf * lax.rsqrt(jnp.mean(xf * xf, axis=-1, keepdims=True) + RMS_EPS) * g.astype(F32)


def _rope(x):
    L = x.shape[1]
    half = x.shape[-1] // 2
    inv = ROPE_THETA ** (-jnp.arange(half, dtype=F32) / half)
    ang = jnp.arange(L, dtype=F32)[:, None] * inv[None, :]
    bshape = (L,) + (1,) * (x.ndim - 3) + (half,)
    cos = jnp.cos(ang).reshape(bshape)
    sin = jnp.sin(ang).reshape(bshape)
    x1, x2 = x[..., :half], x[..., half:]
    return jnp.concatenate([x1 * cos - x2 * sin, x2 * cos + x1 * sin], axis=-1)


def _banded_attention(q, k, v, half_window, block, sink=None):
    n, L, hk, g, dh = q.shape
    nb = -(-L // block)
    lp = nb * block
    ns = -(-half_window // block)
    qp = jnp.pad(q, ((0, 0), (0, lp - L), (0, 0), (0, 0), (0, 0))).reshape(n, nb, block, hk, g, dh)
    kpad = ((0, 0), (ns * block, lp - L + ns * block), (0, 0), (0, 0))
    kb = jnp.pad(k, kpad).reshape(n, nb + 2 * ns, block, hk, dh)
    vb = jnp.pad(v, kpad).reshape(n, nb + 2 * ns, block, hk, dh)
    kw = jnp.concatenate([kb[:, j:j + nb] for j in range(2 * ns + 1)], axis=2)
    vw = jnp.concatenate([vb[:, j:j + nb] for j in range(2 * ns + 1)], axis=2)
    width = (2 * ns + 1) * block
    qpos = jnp.arange(nb)[:, None] * block + jnp.arange(block)[None, :]
    kpos = jnp.arange(nb)[:, None] * block - ns * block + jnp.arange(width)[None, :]
    rel = kpos[:, None, :] - qpos[:, :, None]
    valid = (jnp.abs(rel) <= half_window) & (kpos[:, None, :] >= 0) & (kpos[:, None, :] < L)
    s = jnp.einsum('nbqhgd,nbkhd->nbhgqk', qp.astype(BF16), kw.astype(BF16),
                   preferred_element_type=F32) * (dh ** -0.5)
    s = jnp.where(valid[None, :, None, None], s, NEG_INF)
    m = jnp.max(s, axis=-1)
    if sink is not None:
        sk = sink.astype(F32).reshape(hk, g)[None, None, :, :, None]
        m = jnp.maximum(m, sk)
    p = jnp.exp(s - m[..., None])
    l = jnp.sum(p, axis=-1)
    if sink is not None:
        l = l + jnp.exp(sk - m)
    o = jnp.einsum('nbhgqk,nbkhd->nbhgqd', p.astype(BF16), vw.astype(BF16),
                   preferred_element_type=F32) / l[..., None]
    o = o.transpose(0, 1, 4, 2, 3, 5).reshape(n, lp, hk, g, dh)[:, :L]
    logden = (m + jnp.log(l)).transpose(0, 1, 4, 2, 3).reshape(n, lp, hk, g)[:, :L]
    return o, logden


def _stride_gather(t, dil):
    b, L, h, dh = t.shape
    return t.reshape(b, L // dil, dil, h, dh).transpose(0, 2, 1, 3, 4).reshape(b * dil, L // dil, h, dh)


def _stride_scatter(t, b, dil):
    ls = t.shape[1]
    rest = t.shape[2:]
    t = t.reshape((b, dil, ls) + rest)
    perm = (0, 2, 1) + tuple(range(3, t.ndim))
    return t.transpose(perm).reshape((b, ls * dil) + rest)


def _dilated_mixture(q, k, v):
    b, L = q.shape[:2]
    outs, logdens = [], []
    for gi, (window, dil) in enumerate(DIL_CONFIGS):
        qs = _stride_gather(q[:, :, gi], dil)[:, :, :, None]
        ks = _stride_gather(k[:, :, gi], dil)
        vs = _stride_gather(v[:, :, gi], dil)
        o, ld = _banded_attention(qs, ks, vs, window // (2 * dil), DIL_BLOCK)
        outs.append(_stride_scatter(o[:, :, :, 0], b, dil))
        logdens.append(_stride_scatter(ld[..., 0], b, dil))
    wts = jax.nn.softmax(jnp.stack(logdens, axis=0), axis=0)
    out = jnp.sum(wts[..., None] * jnp.stack(outs, axis=0), axis=0)
    return out.reshape(b, L, DIL_HEADS * DIL_HEAD_DIM)


def _hyena_spectra(L, w1, b1, w2, b2, w3, b3, freq, w_out):
    t = jnp.linspace(0.0, 1.0, L, dtype=F32)
    bands = (FILT_EMB - 1) // 2
    fb = jnp.linspace(1e-4, bands - 1, bands, dtype=F32)
    ang = (2.0 * math.pi / L) * jnp.arange(L, dtype=F32)[:, None] * fb[None, :]
    z = jnp.concatenate([t[:, None], jnp.cos(ang), -jnp.sin(ang)], axis=-1)
    hp = lax.Precision.HIGHEST
    hdn = jnp.sin(freq * (jnp.dot(z, w1, precision=hp) + b1))
    hdn = jnp.sin(freq * (jnp.dot(hdn, w2, precision=hp) + b2))
    hdn = jnp.sin(freq * (jnp.dot(hdn, w3, precision=hp) + b3))
    kern = jnp.dot(hdn, w_out, precision=hp).reshape(L, 2, HY_ORDER, HY_WIDTH)
    deltas = jnp.abs(jnp.linspace(MIN_DECAY, MAX_DECAY, HY_WIDTH, dtype=F32))
    kern = kern * jnp.exp(-t[:, None] * deltas[None, :])[:, None, None, :]
    kf, kb = kern[:, 0], kern[:, 1]
    kfull = jnp.concatenate([kf, jnp.zeros((1, HY_ORDER, HY_WIDTH), F32), kb[:0:-1]], axis=0)
    kfull = kfull / jnp.sum(jnp.abs(kfull), axis=0, keepdims=True)
    return jnp.fft.rfft(kfull, axis=0)


def _long_conv(z, spec):
    L = z.shape[1]
    zf = jnp.fft.rfft(z, n=2 * L, axis=1)
    return jnp.fft.irfft(zf * spec[None], n=2 * L, axis=1)[:, :L]


def _hyena_mixer(u_proj, conv_w, conv_b, spec, bias):
    L = u_proj.shape[1]
    up = jnp.pad(u_proj, ((0, 0), (1, 1), (0, 0)))
    u = conv_b
    for j in range(SHORT_CONV):
        u = u + up[:, j:j + L] * conv_w[j]
    parts = jnp.split(u, HY_ORDER + 1, axis=-1)
    z = parts[0]
    for o in range(HY_ORDER):
        z = parts[o + 1] * (_long_conv(z, spec[:, o]) + bias[o] * z)
    return z


def _mixers(proj, b, L, p, l, spec):
    pr = proj.reshape(b, L, IN_COLS)
    hy_in = pr[..., C_HY:C_HY + HY_COLS].astype(F32)
    y_hy = _hyena_mixer(hy_in, p['hy_conv_w'][l], p['hy_conv_b'][l], spec, p['hy_bias'][l])
    dshape = (b, L, DIL_GROUPS, DIL_HEADS, DIL_HEAD_DIM)
    dq = pr[..., C_DQ:C_DQ + DIL_COLS].reshape(dshape)
    dk = pr[..., C_DK:C_DK + DIL_COLS].reshape(dshape)
    dv = pr[..., C_DV:C_DV + DIL_COLS].reshape(dshape)
    dqn = _rope(_rmsnorm(dq, p['dil_q_norm'][l][:, None, :]))
    dkn = _rope(_rmsnorm(dk, p['dil_k_norm'][l][:, None, :]))
    y_dil = _dilated_mixture(dqn, dkn, dv)
    wq = pr[..., C_WQ:C_WQ + WIN_Q_COLS].reshape(b, L, WIN_Q_HEADS, WIN_HEAD_DIM)
    wk = pr[..., C_WK:C_WK + WIN_KV_COLS].reshape(b, L, WIN_KV_HEADS, WIN_HEAD_DIM)
    wv = pr[..., C_WV:C_WV + WIN_KV_COLS].reshape(b, L, WIN_KV_HEADS, WIN_HEAD_DIM)
    wqn = _rope(_rmsnorm(wq, p['win_q_norm'][l]))
    wkn = _rope(_rmsnorm(wk, p['win_k_norm'][l]))
    o_win, _ = _banded_attention(wqn.reshape(b, L, WIN_KV_HEADS, WIN_GROUP, WIN_HEAD_DIM), wkn, wv,
                                 WIN_HALF, WIN_BLOCK, sink=p['win_sink'][l])
    y_win = o_win.reshape(b, L, WIN_Q_COLS)
    t = b * L
    return tuple(y.reshape(t, BRANCH_WIDTH).astype(BF16) for y in (y_hy, y_dil, y_win))


def _prep_weights(p):
    w_in = p['w_in']
    o_dq = HY_COLS
    o_wq = o_dq + 3 * DIL_COLS
    o_gate = o_wq + WIN_Q_COLS + 2 * WIN_KV_COLS
    w_in_r = jnp.concatenate([w_in[..., o_gate:], w_in[..., :o_gate]], axis=-1).astype(BF16)
    q = dict(p)
    q['w_in_r'] = w_in_r
    for name in ('w_branch', 'w_out', 'dense_w1', 'dense_w3', 'dense_w2', 'moe_w1', 'moe_w3', 'moe_w2'):
        q[name + '_b'] = p[name].astype(BF16)
    return q


def _run_trunk(x, p, interpret=False):
    b, L, d = x.shape
    t = b * L
    xt = x.reshape(t, d)
    for l in range(DEPTH):
        spec = _hyena_spectra(L, p['filt_w1'][l], p['filt_b1'][l], p['filt_w2'][l], p['filt_b2'][l],
                              p['filt_w3'][l], p['filt_b3'][l], p['filt_freq'][l], p['filt_out'][l])
        proj = norm_proj(xt, p['mix_norm'][l], p['w_in_r'][l], interpret=interpret)
        ys = _mixers(proj, b, L, p, l, spec)
        xt = merge_out(xt, proj, ys, p['w_branch_b'][l], p['w_out_b'][l], interpret=interpret)
        j = l // 2
        if l % 2 == 0:
            xt = ffn_dense(xt, p['ffn_norm'][l], p['dense_w1_b'][j], p['dense_w3_b'][j], p['dense_w2_b'][j],
                           interpret=interpret)
        else:
            xt = ffn_moe(xt, p['ffn_norm'][l], p['router_w'][j], p['moe_w1_b'][j], p['moe_w3_b'][j],
                         p['moe_w2_b'][j], interpret=interpret)
    return xt.reshape(b, L, d)


def kernel(x_prompt, x_sample, mix_norm, w_in, hy_conv_w, hy_conv_b, filt_w1, filt_b1, filt_w2, filt_b2, filt_w3, filt_b3, filt_freq, filt_out, hy_bias, dil_q_norm, dil_k_norm, win_q_norm, win_k_norm, win_sink, w_branch, w_out, ffn_norm, dense_w1, dense_w3, dense_w2, router_w, moe_w1, moe_w3, moe_w2):
    p = dict(mix_norm=mix_norm, w_in=w_in, hy_conv_w=hy_conv_w, hy_conv_b=hy_conv_b,
             filt_w1=filt_w1, filt_b1=filt_b1, filt_w2=filt_w2, filt_b2=filt_b2, filt_w3=filt_w3,
             filt_b3=filt_b3, filt_freq=filt_freq, filt_out=filt_out, hy_bias=hy_bias,
             dil_q_norm=dil_q_norm, dil_k_norm=dil_k_norm, win_q_norm=win_q_norm,
             win_k_norm=win_k_norm, win_sink=win_sink, w_branch=w_branch, w_out=w_out,
             ffn_norm=ffn_norm, dense_w1=dense_w1, dense_w3=dense_w3, dense_w2=dense_w2,
             router_w=router_w, moe_w1=moe_w1, moe_w3=moe_w3, moe_w2=moe_w2)
    p = _prep_weights(p)
    return (_run_trunk(x_prompt, p), _run_trunk(x_sample, p))
```
